```python
import jax, jax.numpy as jnp
from jax import lax
import numpy as np

D_MODEL = 1024
BATCH = 8
SEQ = 2048
DEPTH = 2

D_MIX = D_MODEL
HEAD_DIM = 64
D_CONV = D_MIX // 4
CONV_GROUPS = D_CONV // HEAD_DIM
D_SGU = D_MIX // 4
SGU_HEADS = D_SGU // HEAD_DIM
D_SB = D_MIX - D_CONV - D_SGU
SB_HEADS = D_SB // HEAD_DIM
D_IN = 2 * D_CONV + 2 * D_SGU + 3 * D_SB
CONV_K = 31
CHUNK = 128
Q_BLOCK = 128
FFN_CONV_K = 3
D_FF = ((8 * D_MODEL // 3 + 127) // 128) * 128
EPS = 1e-6

kernel_name = "hybrid_conv_sgu_stickbreak_block"


def _rms(x):
    xf = x.astype(jnp.float32)
    return xf * lax.rsqrt(jnp.mean(xf * xf, axis=-1, keepdims=True) + EPS)


def rmsnorm(x, g):
    return (_rms(x) * g.astype(jnp.float32)).astype(x.dtype)


def layernorm(x, g, b):
    xf = x.astype(jnp.float32)
    mu = jnp.mean(xf, axis=-1, keepdims=True)
    var = jnp.mean(jnp.square(xf - mu), axis=-1, keepdims=True)
    y = (xf - mu) * lax.rsqrt(var + EPS)
    return (y * g.astype(jnp.float32) + b.astype(jnp.float32)).astype(x.dtype)


def causal_dwconv(x, w, b):
    c = x.shape[-1]
    k = w.shape[0]
    y = lax.conv_general_dilated(
        x, w[:, None, :].astype(x.dtype), window_strides=(1,), padding=[(k - 1, 0)],
        dimension_numbers=("NWC", "WIO", "NWC"), feature_group_count=c)
    return y + b.astype(x.dtype)


def conformer_conv(a_val, a_gate, conv_w, conv_b, ln_g, ln_b):
    h = a_val * jax.nn.sigmoid(a_gate)
    h = causal_dwconv(h, conv_w, conv_b)
    h = layernorm(h, ln_g, ln_b)
    return jax.nn.silu(h)


def chunked_sgu(u, v, ln_g, ln_b, w_s, b_s):
    bsz, s, c = u.shape
    u = jax.nn.gelu(u, approximate=False)
    v = layernorm(jax.nn.gelu(v, approximate=False), ln_g, ln_b)
    v = v.reshape(bsz, s // CHUNK, CHUNK, SGU_HEADS, HEAD_DIM)
    mask = jnp.tril(jnp.ones((CHUNK, CHUNK), dtype=bool))
    ws = jnp.where(mask[None], w_s, 0.0).astype(v.dtype)
    mixed = jnp.einsum("hts,bnshd->bnthd", ws, v)
    mixed = mixed + b_s.T.astype(v.dtype)[None, None, :, :, None]
    return u * mixed.reshape(bsz, s, c)


def stick_breaking_attention(q, k, v):
    bsz, s, h, d = q.shape
    scale = d ** -0.5
    q = q.transpose(0, 2, 1, 3)
    k = k.transpose(0, 2, 1, 3)
    v = v.transpose(0, 2, 1, 3)
    outs = []
    for i in range(s // Q_BLOCK):
        q0 = i * Q_BLOCK
        n = q0 + Q_BLOCK
        qb = q[:, :, q0:n]
        kb = k[:, :, :n]
        vb = v[:, :, :n]
        z = jnp.einsum("bhqd,bhkd->bhqk", qb, kb).astype(jnp.float32) * scale
        t_pos = q0 + jnp.arange(Q_BLOCK)[:, None]
        s_pos = jnp.arange(n)[None, :]
        mask = s_pos < t_pos
        log_1m_beta = jnp.where(mask, -jax.nn.softplus(z), 0.0)
        log_beta = -jax.nn.softplus(-z)
        log_a = log_beta + lax.cumsum(log_1m_beta, axis=3, reverse=True) - log_1m_beta
        a = jnp.where(mask, jnp.exp(log_a), 0.0)
        outs.append(jnp.einsum("bhqk,bhkd->bhqd", a.astype(vb.dtype), vb))
    o = jnp.concatenate(outs, axis=2).transpose(0, 2, 1, 3)
    return o.reshape(bsz, s, h * d)


def setup_inputs(seed: int = 0) -> dict:
    key = jax.random.key(seed)
    ks = jax.random.split(key, 20)
    f32 = jnp.float32

    def nrm(k, shape, scale):
        return jax.random.normal(k, shape, f32) * scale

    return {
        "x": nrm(ks[0], (BATCH, SEQ, D_MODEL), 1.0),
        "g_mix": 1.0 + nrm(ks[1], (DEPTH, D_MODEL), 0.02),
        "w_in": nrm(ks[2], (DEPTH, D_MODEL, D_IN), D_MODEL ** -0.5),
        "conv_w": nrm(ks[3], (DEPTH, CONV_K, D_CONV), CONV_K ** -0.5),
        "conv_b": nrm(ks[4], (DEPTH, D_CONV), 0.02),
        "conv_ln_g": 1.0 + nrm(ks[5], (DEPTH, D_CONV), 0.02),
        "conv_ln_b": nrm(ks[6], (DEPTH, D_CONV), 0.02),
        "sgu_ln_g": 1.0 + nrm(ks[7], (DEPTH, D_SGU), 0.02),
        "sgu_ln_b": nrm(ks[8], (DEPTH, D_SGU), 0.02),
        "sgu_w": nrm(ks[9], (DEPTH, SGU_HEADS, CHUNK, CHUNK), 0.5 * CHUNK ** -0.5),
        "sgu_b": 1.0 + nrm(ks[10], (DEPTH, SGU_HEADS, CHUNK), 0.02),
        "g_out": 1.0 + nrm(ks[11], (DEPTH, D_MIX), 0.02),
        "w_out": nrm(ks[12], (DEPTH, D_MIX, D_MODEL), D_MIX ** -0.5),
        "g_ffn": 1.0 + nrm(ks[13], (DEPTH, D_MODEL), 0.02),
        "w_up": nrm(ks[14], (DEPTH, D_MODEL, 2 * D_FF), D_MODEL ** -0.5),
        "ffn_conv_w": nrm(ks[15], (DEPTH, FFN_CONV_K, 2 * D_FF), FFN_CONV_K ** -0.5),
        "ffn_conv_b": nrm(ks[16], (DEPTH, 2 * D_FF), 0.02),
        "w_down": nrm(ks[17], (DEPTH, D_FF, D_MODEL), D_FF ** -0.5),
        "g_final": 1.0 + nrm(ks[18], (D_MODEL,), 0.02),
    }


def reference(x, g_mix, w_in, conv_w, conv_b, conv_ln_g, conv_ln_b, sgu_ln_g, sgu_ln_b,
              sgu_w, sgu_b, g_out, w_out, g_ffn, w_up, ffn_conv_w, ffn_conv_b, w_down,
              g_final):
    bsz, s, _ = x.shape
    splits = np.cumsum([D_CONV, D_CONV, D_SGU, D_SGU, D_SB, D_SB]).tolist()
    for l in range(DEPTH):
        h = rmsnorm(x, g_mix[l])
        p = h @ w_in[l]
        a_val, a_gate, b_u, b_v, c_q, c_k, c_v = jnp.split(p, splits, axis=-1)
        y_a = conformer_conv(a_val, a_gate, conv_w[l], conv_b[l], conv_ln_g[l], conv_ln_b[l])
        y_b = chunked_sgu(b_u, b_v, sgu_ln_g[l], sgu_ln_b[l], sgu_w[l], sgu_b[l])
        hs = (bsz, s, SB_HEADS, HEAD_DIM)
        y_c = stick_breaking_attention(c_q.reshape(hs), c_k.reshape(hs), c_v.reshape(hs))
        y = jnp.concatenate([_rms(y_a), _rms(y_b), _rms(y_c)], axis=-1)
        y = (y * g_out[l].astype(jnp.float32)).astype(x.dtype)
        x = x + y @ w_out[l]
        h = rmsnorm(x, g_ffn[l])
        up = causal_dwconv(h @ w_up[l], ffn_conv_w[l], ffn_conv_b[l])
        gate, val = jnp.split(up, 2, axis=-1)
        x = x + (jax.nn.silu(gate) * val) @ w_down[l]
    return rmsnorm(x, g_final)
```

```python
import functools

import jax
import jax.numpy as jnp
from jax import lax
from jax.experimental import pallas as pl
from jax.experimental.pallas import tpu as pltpu

EPS = 1e-6
HEAD_DIM = 64
CONV_K = 31
SGU_CHUNK = 128
FFN_CONV_K = 3

LANES = 128
SUBLANES = 8
VMEM_LIMIT_BYTES = 56 * 1024 * 1024

MIX_TILE = 512
CONV_HALO = 32
CONV_ROWS = 64
ATT_TILE = 256
FFN_TILE = 512
FFN_CHUNK = 256
FFN_HALO = SUBLANES

_BF16 = jnp.bfloat16
_F32 = jnp.float32


def _rms_scale(x):
    return lax.rsqrt(jnp.mean(x * x, axis=-1, keepdims=True) + EPS)


def _layernorm(x, g, b):
    mu = jnp.mean(x, axis=-1, keepdims=True)
    xc = x - mu
    var = jnp.mean(xc * xc, axis=-1, keepdims=True)
    return xc * lax.rsqrt(var + EPS) * g + b


def _gelu(x):
    return 0.5 * x * (1.0 + lax.erf(x * (2.0 ** -0.5)))


def _dot(a, b):
    return jnp.dot(a, b, preferred_element_type=_F32)


def _mix_in_kernel(x_ref, g_ref, w_ref, cw_ref, cb_ref, clg_ref, clb_ref, slg_ref, slb_ref,
                   sw_ref, sb_ref, go_ref,
                   yab_ref, q_ref, k_ref, v_ref,
                   hbuf_ref, conv_ref, *, d_conv, d_sgu, d_sb):
    t = pl.program_id(1)
    ts = x_ref.shape[0]

    x = x_ref[...]
    h = (x * _rms_scale(x) * g_ref[...]).astype(_BF16)

    pa = _dot(h, w_ref[:, 0:2 * d_conv])
    glu = pa[:, :d_conv] * jax.nn.sigmoid(pa[:, d_conv:])

    @pl.when(t == 0)
    def _():
        hbuf_ref[0:CONV_HALO, :] = jnp.zeros((CONV_HALO, d_conv), _F32)

    hbuf_ref[CONV_HALO:CONV_HALO + ts, :] = glu
    base = CONV_HALO - (CONV_K - 1)
    for r0 in range(0, ts, CONV_ROWS):
        acc = jnp.broadcast_to(cb_ref[...], (CONV_ROWS, d_conv))
        for kk in range(CONV_K):
            acc = acc + cw_ref[kk:kk + 1, :] * hbuf_ref[base + r0 + kk:base + r0 + kk + CONV_ROWS, :]
        conv_ref[r0:r0 + CONV_ROWS, :] = acc
    hbuf_ref[0:CONV_HALO, :] = hbuf_ref[ts:ts + CONV_HALO, :]
    ya = _layernorm(conv_ref[...], clg_ref[...], clb_ref[...])
    ya = ya * jax.nn.sigmoid(ya)
    yab_ref[:, 0:d_conv] = (ya * _rms_scale(ya) * go_ref[:, 0:d_conv]).astype(yab_ref.dtype)

    pb = _dot(h, w_ref[:, 2 * d_conv:2 * d_conv + 2 * d_sgu])
    u = _gelu(pb[:, :d_sgu])
    vn = _layernorm(_gelu(pb[:, d_sgu:]), slg_ref[...], slb_ref[...]).astype(_BF16)
    n_heads = sw_ref.shape[0]
    row = lax.broadcasted_iota(jnp.int32, (SGU_CHUNK, SGU_CHUNK), 0)
    col = lax.broadcasted_iota(jnp.int32, (SGU_CHUNK, SGU_CHUNK), 1)
    ws = [jnp.where(col <= row, sw_ref[hh], 0.0).astype(_BF16) for hh in range(n_heads)]
    head_of_lane = lax.broadcasted_iota(jnp.int32, (SGU_CHUNK, d_sgu), 1) // HEAD_DIM
    mixed = []
    for c0 in range(0, ts, SGU_CHUNK):
        vc = vn[c0:c0 + SGU_CHUNK, :]
        m = sb_ref[...]
        for hh in range(n_heads):
            m = m + jnp.where(head_of_lane == hh, _dot(ws[hh], vc), 0.0)
        mixed.append(m)
    yb = u * jnp.concatenate(mixed, axis=0)
    yab_ref[:, d_conv:d_conv + d_sgu] = (
        yb * _rms_scale(yb) * go_ref[:, d_conv:d_conv + d_sgu]).astype(yab_ref.dtype)

    c0 = 2 * d_conv + 2 * d_sgu
    q_ref[...] = (_dot(h, w_ref[:, c0:c0 + d_sb]) * (HEAD_DIM ** -0.5)).astype(q_ref.dtype)
    k_ref[...] = _dot(h, w_ref[:, c0 + d_sb:c0 + 2 * d_sb]).astype(k_ref.dtype)
    v_ref[...] = _dot(h, w_ref[:, c0 + 2 * d_sb:c0 + 3 * d_sb]).astype(v_ref.dtype)


def _mix_in(x, g_mix, w_in, conv_w, conv_b, cln_g, cln_b, sln_g, sln_b, sgu_w, sgu_bias, g_out_ab,
            d_conv, d_sgu, d_sb):
    bsz, s, d = x.shape
    ts = MIX_TILE
    assert s % ts == 0 and ts % SGU_CHUNK == 0 and ts % CONV_ROWS == 0
    const = lambda shape: pl.BlockSpec(shape, lambda b, t: (0,) * len(shape))
    tok = lambda width: pl.BlockSpec((None, ts, width), lambda b, t: (b, t, 0))
    kern = functools.partial(_mix_in_kernel, d_conv=d_conv, d_sgu=d_sgu, d_sb=d_sb)
    return pl.pallas_call(
        kern,
        grid=(bsz, s // ts),
        in_specs=[tok(d), const((1, d)), const(w_in.shape), const(conv_w.shape), const((1, d_conv)),
                  const((1, d_conv)), const((1, d_conv)), const((1, d_sgu)), const((1, d_sgu)),
                  const(sgu_w.shape), const(sgu_bias.shape), const((1, d_conv + d_sgu))],
        out_specs=[tok(d_conv + d_sgu), tok(d_sb), tok(d_sb), tok(d_sb)],
        out_shape=[jax.ShapeDtypeStruct((bsz, s, d_conv + d_sgu), _BF16)]
        + [jax.ShapeDtypeStruct((bsz, s, d_sb), _BF16)] * 3,
        scratch_shapes=[pltpu.VMEM((CONV_HALO + ts, d_conv), _F32), pltpu.VMEM((ts, d_conv), _F32)],
        compiler_params=pltpu.CompilerParams(
            dimension_semantics=("arbitrary", "arbitrary"), vmem_limit_bytes=VMEM_LIMIT_BYTES),
        name="mix_in",
    )(x, g_mix, w_in, conv_w, conv_b, cln_g, cln_b, sln_g, sln_b, sgu_w, sgu_bias, g_out_ab)


def _attn_tile(qm, kj, vj, tri_ref, diag_mask):
    z = lax.dot_general(qm, kj, (((1,), (1,)), ((), ())), preferred_element_type=_F32)
    lse = jnp.log(1.0 + jnp.exp(-jnp.abs(z)))
    log_beta = jnp.minimum(z, 0.0) - lse
    log_1m = log_beta - z
    if diag_mask is not None:
        log_1m = jnp.where(diag_mask, log_1m, 0.0)
    hi = log_1m.astype(_BF16)
    lo = (log_1m - hi.astype(_F32)).astype(_BF16)
    cum = _dot(jnp.concatenate([hi, lo], axis=1), tri_ref[...])
    a = jnp.exp(log_beta + cum)
    if diag_mask is not None:
        a = jnp.where(diag_mask, a, 0.0)
    r = _dot(a.astype(_BF16), vj)
    tot = cum[:, 0:1] + log_1m[:, 0:1]
    return r, tot


def _attn_kernel(q_ref, k_ref, v_ref, tri_ref, o_ref, acc_ref, carry_ref):
    s = q_ref.shape[0]
    tq = ATT_TILE
    lane = lax.broadcasted_iota(jnp.int32, (tq, LANES), 1)
    head_lanes = [lane < HEAD_DIM, lane >= HEAD_DIM]
    row = lax.broadcasted_iota(jnp.int32, (tq, tq), 0)
    col = lax.broadcasted_iota(jnp.int32, (tq, tq), 1)
    diag_mask = col < row

    def q_tile(i, _):
        q0 = pl.multiple_of(i * tq, tq)
        q2 = q_ref[pl.ds(q0, tq), :]
        qm = [jnp.where(m, q2, jnp.zeros_like(q2)) for m in head_lanes]
        kd = k_ref[pl.ds(q0, tq), :]
        vd = v_ref[pl.ds(q0, tq), :]
        for hh in range(2):
            r, tot = _attn_tile(qm[hh], kd, vd, tri_ref, diag_mask)
            acc_ref[hh] = r
            carry_ref[hh] = jnp.broadcast_to(tot, (tq, LANES))

        def k_tile(jj, _):
            k0 = pl.multiple_of((i - 1 - jj) * tq, tq)
            kj = k_ref[pl.ds(k0, tq), :]
            vj = v_ref[pl.ds(k0, tq), :]
            for hh in range(2):
                r, tot = _attn_tile(qm[hh], kj, vj, tri_ref, None)
                carry = carry_ref[hh]
                acc_ref[hh] += r * jnp.exp(carry)
                carry_ref[hh] = carry + jnp.broadcast_to(tot, (tq, LANES))
            return 0

        lax.fori_loop(0, i, k_tile, 0)
        o_ref[pl.ds(q0, tq), :] = jnp.where(head_lanes[0], acc_ref[0], acc_ref[1])
        return 0

    lax.fori_loop(0, s // tq, q_tile, 0)


def _attn(q, k, v, tri):
    bsz, s, d_sb = q.shape
    assert s % ATT_TILE == 0 and d_sb % LANES == 0 and LANES == 2 * HEAD_DIM
    blk = pl.BlockSpec((None, s, LANES), lambda b, p: (b, 0, p))
    return pl.pallas_call(
        _attn_kernel,
        grid=(bsz, d_sb // LANES),
        in_specs=[blk, blk, blk, pl.BlockSpec(tri.shape, lambda b, p: (0, 0))],
        out_specs=blk,
        out_shape=jax.ShapeDtypeStruct((bsz, s, d_sb), _F32),
        scratch_shapes=[pltpu.VMEM((2, ATT_TILE, LANES), _F32), pltpu.VMEM((2, ATT_TILE, LANES), _F32)],
        compiler_params=pltpu.CompilerParams(
            dimension_semantics=("arbitrary", "arbitrary"), vmem_limit_bytes=VMEM_LIMIT_BYTES),
        name="attn",
    )(q, k, v, tri)


def _out_ffn_kernel(x_ref, yab_ref, oc_ref, goc_ref, wo_ref, gf_ref, wu_ref, fcw_ref, fcb_ref, wd_ref,
                    gfin_ref, out_ref, ubuf_ref, carry_ref, acc_ref, *, final_norm):
    t = pl.program_id(1)
    tm = x_ref.shape[0]
    n_chunks = wu_ref.shape[0]
    ch = wd_ref.shape[1]

    oc = oc_ref[...]
    yc = (oc * _rms_scale(oc) * goc_ref[...]).astype(_BF16)
    y = jnp.concatenate([yab_ref[...], yc], axis=1)
    x1 = x_ref[...] + _dot(y, wo_ref[...])
    h = (x1 * _rms_scale(x1) * gf_ref[...]).astype(_BF16)

    @pl.when(t == 0)
    def _():
        carry_ref[...] = jnp.zeros(carry_ref.shape, _F32)

    acc_ref[...] = x1

    def chunk(c, _):
        up = _dot(h, wu_ref[c])
        ubuf_ref[0:FFN_HALO, :] = carry_ref[c]
        ubuf_ref[FFN_HALO:FFN_HALO + tm, :] = up
        carry_ref[c] = up[tm - FFN_HALO:tm, :]
        w = fcw_ref[c]
        conv = jnp.broadcast_to(fcb_ref[c], (tm, 2 * ch))
        for kk in range(FFN_CONV_K):
            off = FFN_HALO - (FFN_CONV_K - 1) + kk
            conv = conv + w[kk:kk + 1, :] * ubuf_ref[off:off + tm, :]
        gate = conv[:, :ch]
        act = (gate * jax.nn.sigmoid(gate) * conv[:, ch:]).astype(_BF16)
        acc_ref[...] += _dot(act, wd_ref[c])
        return 0

    lax.fori_loop(0, n_chunks, chunk, 0)
    x2 = acc_ref[...]
    if final_norm:
        x2 = x2 * _rms_scale(x2) * gfin_ref[...]
    out_ref[...] = x2


def _out_ffn(x, yab, oc, g_out_c, w_out, g_ffn, w_up_r, fcw_r, fcb_r, w_down_r, g_final, final_norm):
    bsz, s, d = x.shape
    tm = FFN_TILE
    assert s % tm == 0
    n_chunks, _, ch2 = w_up_r.shape
    const = lambda shape: pl.BlockSpec(shape, lambda b, t: (0,) * len(shape))
    tok = lambda width: pl.BlockSpec((None, tm, width), lambda b, t: (b, t, 0))
    kern = functools.partial(_out_ffn_kernel, final_norm=final_norm)
    return pl.pallas_call(
        kern,
        grid=(bsz, s // tm),
        in_specs=[tok(d), tok(yab.shape[-1]), tok(oc.shape[-1]), const(g_out_c.shape), const(w_out.shape),
                  const(g_ffn.shape), const(w_up_r.shape), const(fcw_r.shape), const(fcb_r.shape),
                  const(w_down_r.shape), const(g_final.shape)],
        out_specs=tok(d),
        out_shape=jax.ShapeDtypeStruct((bsz, s, d), _F32),
        scratch_shapes=[pltpu.VMEM((FFN_HALO + tm, ch2), _F32),
                        pltpu.VMEM((n_chunks, FFN_HALO, ch2), _F32),
                        pltpu.VMEM((tm, d), _F32)],
        compiler_params=pltpu.CompilerParams(
            dimension_semantics=("arbitrary", "arbitrary"), vmem_limit_bytes=VMEM_LIMIT_BYTES),
        name="out_ffn",
    )(x, yab, oc, g_out_c, w_out, g_ffn, w_up_r, fcw_r, fcb_r, w_down_r, g_final)


def kernel(x, g_mix, w_in, conv_w, conv_b, conv_ln_g, conv_ln_b, sgu_ln_g, sgu_ln_b, sgu_w, sgu_b,
           g_out, w_out, g_ffn, w_up, ffn_conv_w, ffn_conv_b, w_down, g_final):
    depth = w_in.shape[0]
    d_conv = conv_w.shape[-1]
    d_sgu = sgu_ln_g.shape[-1]
    d_sb = g_out.shape[-1] - d_conv - d_sgu
    d_ff = w_down.shape[1]
    n_chunks = d_ff // FFN_CHUNK
    assert d_ff % FFN_CHUNK == 0

    r = lax.broadcasted_iota(jnp.int32, (ATT_TILE, ATT_TILE), 0)
    c = lax.broadcasted_iota(jnp.int32, (ATT_TILE, ATT_TILE), 1)
    tri = jnp.where(r > c, 1.0, 0.0).astype(_BF16)
    tri = jnp.concatenate([tri, tri], axis=0)

    def chunked(a):
        g = a[..., :d_ff].reshape(a.shape[:-1] + (n_chunks, FFN_CHUNK))
        v = a[..., d_ff:].reshape(a.shape[:-1] + (n_chunks, FFN_CHUNK))
        return jnp.moveaxis(jnp.concatenate([g, v], axis=-1), -2, 0)

    row2 = lambda a: a.reshape(1, -1)
    for l in range(depth):
        sgu_bias = jnp.repeat(sgu_b[l].T, HEAD_DIM, axis=1)
        yab, q, k, v = _mix_in(
            x, row2(g_mix[l]), w_in[l].astype(_BF16), conv_w[l], row2(conv_b[l]), row2(conv_ln_g[l]),
            row2(conv_ln_b[l]), row2(sgu_ln_g[l]), row2(sgu_ln_b[l]), sgu_w[l], sgu_bias,
            row2(g_out[l, :d_conv + d_sgu]), d_conv, d_sgu, d_sb)
        oc = _attn(q, k, v, tri)
        x = _out_ffn(
            x, yab, oc, row2(g_out[l, d_conv + d_sgu:]), w_out[l].astype(_BF16), row2(g_ffn[l]),
            chunked(w_up[l].astype(_BF16)), chunked(ffn_conv_w[l]), chunked(ffn_conv_b[l][None, :]),
            w_down[l].astype(_BF16).reshape(n_chunks, FFN_CHUNK, -1), row2(g_final),
            final_norm=(l == depth - 1))
    return x
```

```python
import functools

import jax
import jax.numpy as jnp
from jax import lax
from jax.experimental import pallas as pl
from jax.experimental.pallas import tpu as pltpu

EPS = 1e-6
HEAD_DIM = 64
CONV_K = 31
SGU_CHUNK = 128
FFN_CONV_K = 3

LANES = 128
SUBLANES = 8
VMEM_LIMIT_BYTES = 56 * 1024 * 1024

MIX_TILE = 512
CONV_HALO = 32
CONV_ROWS = 64
ATT_TILE = 256
FFN_TILE = 512
FFN_CHUNK = 256
FFN_HALO = SUBLANES

_BF16 = jnp.bfloat16
_F32 = jnp.float32


def _rms_scale(x):
    return lax.rsqrt(jnp.mean(x * x, axis=-1, keepdims=True) + EPS)


def _layernorm(x, g, b):
    mu = jnp.mean(x, axis=-1, keepdims=True)
    xc = x - mu
    var = jnp.mean(xc * xc, axis=-1, keepdims=True)
    return xc * lax.rsqrt(var + EPS) * g + b


def _gelu(x):
    return 0.5 * x * (1.0 + lax.erf(x * (2.0 ** -0.5)))


def _dot(a, b):
    return jnp.dot(a, b, preferred_element_type=_F32)


def _mix_in_kernel(x_ref, g_ref, w_ref, cw_ref, cb_ref, clg_ref, clb_ref, slg_ref, slb_ref,
                   sw_ref, sb_ref, go_ref,
                   yab_ref, q_ref, k_ref, v_ref,
                   hbuf_ref, conv_ref, *, d_conv, d_sgu, d_sb):
    t = pl.program_id(1)
    ts = x_ref.shape[0]

    x = x_ref[...]
    h = (x * _rms_scale(x) * g_ref[...]).astype(_BF16)

    pa = _dot(h, w_ref[:, 0:2 * d_conv])
    glu = pa[:, :d_conv] * jax.nn.sigmoid(pa[:, d_conv:])

    @pl.when(t == 0)
    def _():
        hbuf_ref[0:CONV_HALO, :] = jnp.zeros((CONV_HALO, d_conv), _F32)

    hbuf_ref[CONV_HALO:CONV_HALO + ts, :] = glu
    base = CONV_HALO - (CONV_K - 1)
    for r0 in range(0, ts, CONV_ROWS):
        acc = jnp.broadcast_to(cb_ref[...], (CONV_ROWS, d_conv))
        for kk in range(CONV_K):
            acc = acc + cw_ref[kk:kk + 1, :] * hbuf_ref[base + r0 + kk:base + r0 + kk + CONV_ROWS, :]
        conv_ref[r0:r0 + CONV_ROWS, :] = acc
    hbuf_ref[0:CONV_HALO, :] = hbuf_ref[ts:ts + CONV_HALO, :]
    ya = _layernorm(conv_ref[...], clg_ref[...], clb_ref[...])
    ya = ya * jax.nn.sigmoid(ya)
    yab_ref[:, 0:d_conv] = (ya * _rms_scale(ya) * go_ref[:, 0:d_conv]).astype(yab_ref.dtype)

    pb = _dot(h, w_ref[:, 2 * d_conv:2 * d_conv + 2 * d_sgu])
    u = _gelu(pb[:, :d_sgu])
    vn = _layernorm(_gelu(pb[:, d_sgu:]), slg_ref[...], slb_ref[...]).astype(_BF16)
    n_heads = sw_ref.shape[0]
    row = lax.broadcasted_iota(jnp.int32, (SGU_CHUNK, SGU_CHUNK), 0)
    col = lax.broadcasted_iota(jnp.int32, (SGU_CHUNK, SGU_CHUNK), 1)
    ws = [jnp.where(col <= row, sw_ref[hh], 0.0).astype(_BF16) for hh in range(n_heads)]
    head_of_lane = lax.broadcasted_iota(jnp.int32, (SGU_CHUNK, d_sgu), 1) // HEAD_DIM
    mixed = []
    for c0 in range(0, ts, SGU_CHUNK):
        vc = vn[c0:c0 + SGU_CHUNK, :]
        m = sb_ref[...]
        for hh in range(n_heads):
            m = m + jnp.where(head_of_lane == hh, _dot(ws[hh], vc), 0.0)
        mixed.append(m)
    yb = u * jnp.concatenate(mixed, axis=0)
    yab_ref[:, d_conv:d_conv + d_sgu] = (
        yb * _rms_scale(yb) * go_ref[:, d_conv:d_conv + d_sgu]).astype(yab_ref.dtype)

    c0 = 2 * d_conv + 2 * d_sgu
    q_ref[...] = (_dot(h, w_ref[:, c0:c0 + d_sb]) * (HEAD_DIM ** -0.5)).astype(q_ref.dtype)
    k_ref[...] = _dot(h, w_ref[:, c0 + d_sb:c0 + 2 * d_sb]).astype(k_ref.dtype)
    v_ref[...] = _dot(h, w_ref[:, c0 + 2 * d_sb:c0 + 3 * d_sb]).astype(v_ref.dtype)


def _mix_in(x, g_mix, w_in, conv_w, conv_b, cln_g, cln_b, sln_g, sln_b, sgu_w, sgu_bias, g_out_ab,
            d_conv, d_sgu, d_sb):
    bsz, s, d = x.shape
    ts = MIX_TILE
    assert s % ts == 0 and ts % SGU_CHUNK == 0 and ts % CONV_ROWS == 0
    const = lambda shape: pl.BlockSpec(shape, lambda b, t: (0,) * len(shape))
    tok = lambda width: pl.BlockSpec((None, ts, width), lambda b, t: (b, t, 0))
    kern = functools.partial(_mix_in_kernel, d_conv=d_conv, d_sgu=d_sgu, d_sb=d_sb)
    return pl.pallas_call(
        kern,
        grid=(bsz, s // ts),
        in_specs=[tok(d), const((1, d)), const(w_in.shape), const(conv_w.shape), const((1, d_conv)),
                  const((1, d_conv)), const((1, d_conv)), const((1, d_sgu)), const((1, d_sgu)),
                  const(sgu_w.shape), const(sgu_bias.shape), const((1, d_conv + d_sgu))],
        out_specs=[tok(d_conv + d_sgu), tok(d_sb), tok(d_sb), tok(d_sb)],
        out_shape=[jax.ShapeDtypeStruct((bsz, s, d_conv + d_sgu), _BF16)]
        + [jax.ShapeDtypeStruct((bsz, s, d_sb), _BF16)] * 3,
        scratch_shapes=[pltpu.VMEM((CONV_HALO + ts, d_conv), _F32), pltpu.VMEM((ts, d_conv), _F32)],
        compiler_params=pltpu.CompilerParams(
            dimension_semantics=("arbitrary", "arbitrary"), vmem_limit_bytes=VMEM_LIMIT_BYTES),
        name="mix_in",
    )(x, g_mix, w_in, conv_w, conv_b, cln_g, cln_b, sln_g, sln_b, sgu_w, sgu_bias, g_out_ab)


_MASKED_SCORE = -1e30


def _att_stage1(i, j, q_ref, k_ref, cmr_ref, lb_ref, hl_ref, tot_ref):
    tq = ATT_TILE
    q2 = q_ref[pl.ds(pl.multiple_of(i * tq, tq), tq), :]
    kj = k_ref[pl.ds(pl.multiple_of(j * tq, tq), tq), :]
    lane = lax.broadcasted_iota(jnp.int32, (tq, LANES), 1)
    mask = cmr_ref[...] < (i - j) * tq
    for hh in range(2):
        in_head = (lane < HEAD_DIM) if hh == 0 else (lane >= HEAD_DIM)
        qm = jnp.where(in_head, q2, jnp.zeros_like(q2))
        z = lax.dot_general(qm, kj, (((1,), (1,)), ((), ())), preferred_element_type=_F32)
        z = jnp.where(mask, z, _MASKED_SCORE)
        lse = jnp.log(1.0 + jnp.exp(-jnp.abs(z)))
        log_beta = jnp.minimum(z, 0.0) - lse
        log_1m = log_beta - z
        hi = log_1m.astype(_BF16)
        lo = (log_1m - hi.astype(_F32)).astype(_BF16)
        lb_ref[hh] = log_beta
        hl_ref[hh, :, 0:tq] = hi
        hl_ref[hh, :, tq:2 * tq] = lo
        tot_ref[hh] = jnp.broadcast_to(jnp.sum(log_1m, axis=1, keepdims=True), (tq, LANES))


def _att_stage2(tri_ref, lb_ref, hl_ref, tot_ref, a_ref, tot2_ref):
    for hh in range(2):
        cum = _dot(hl_ref[hh], tri_ref[...])
        a_ref[hh] = jnp.exp(lb_ref[hh] + cum).astype(_BF16)
        tot2_ref[hh] = tot_ref[hh]


def _att_stage3(i, j, v_ref, a_ref, tot2_ref, acc_ref, carry_ref, o_ref):
    tq = ATT_TILE
    vj = v_ref[pl.ds(pl.multiple_of(j * tq, tq), tq), :]
    first = i == j
    for hh in range(2):
        r = _dot(a_ref[hh], vj)
        carry = jnp.where(first, 0.0, carry_ref[hh])
        acc = jnp.where(first, 0.0, acc_ref[hh])
        acc_ref[hh] = acc + r * jnp.exp(carry)
        carry_ref[hh] = carry + tot2_ref[hh]
    lane = lax.broadcasted_iota(jnp.int32, (tq, LANES), 1)
    o_ref[pl.ds(pl.multiple_of(i * tq, tq), tq), :] = jnp.where(lane < HEAD_DIM, acc_ref[0], acc_ref[1])


def _next_item(i, j):
    wrap = j == 0
    return jnp.where(wrap, i + 1, i), jnp.where(wrap, i + 1, j - 1)


def _attn_kernel(q_ref, k_ref, v_ref, tri_ref, o_ref,
                 cmr_ref, lb0, lb1, hl0, hl1, t10, t11, a0, a1, t20, t21, acc_ref, carry_ref):
    tq = ATT_TILE
    nq = q_ref.shape[0] // tq
    n_items = nq * (nq + 1) // 2
    cmr_ref[...] = (lax.broadcasted_iota(jnp.int32, (tq, tq), 1)
                    - lax.broadcasted_iota(jnp.int32, (tq, tq), 0))
    acc_ref[...] = jnp.zeros(acc_ref.shape, _F32)
    carry_ref[...] = jnp.zeros(carry_ref.shape, _F32)
    lb, hl, t1, a, t2 = (lb0, lb1), (hl0, hl1), (t10, t11), (a0, a1), (t20, t21)

    s1 = lambda ij, p: _att_stage1(ij[0], ij[1], q_ref, k_ref, cmr_ref, lb[p], hl[p], t1[p])
    s2 = lambda p: _att_stage2(tri_ref, lb[p], hl[p], t1[p], a[p], t2[p])
    s3 = lambda ij, p: _att_stage3(ij[0], ij[1], v_ref, a[p], t2[p], acc_ref, carry_ref, o_ref)

    zero = jnp.int32(0)
    it0 = (zero, zero)
    it1 = _next_item(*it0)
    s1(it0, 0)
    s1(it1, 1)
    s2(0)

    def body(_, c):
        cur, nxt = (c[0], c[1]), (c[2], c[3])
        s3(cur, 0)
        s2(1)
        s1(nxt, 0)
        cur, nxt = _next_item(*cur), _next_item(*nxt)
        s3(cur, 1)
        s2(0)
        s1(nxt, 1)
        cur, nxt = _next_item(*cur), _next_item(*nxt)
        return (*cur, *nxt)

    it2 = _next_item(*it1)
    c = lax.fori_loop(0, n_items // 2 - 1, body, (*it0, *it2))
    cur = (c[0], c[1])
    s3(cur, 0)
    s2(1)
    s3(_next_item(*cur), 1)


def _attn(q, k, v, tri):
    bsz, s, d_sb = q.shape
    tq = ATT_TILE
    nq = s // tq
    assert s % tq == 0 and d_sb % LANES == 0 and LANES == 2 * HEAD_DIM
    assert (nq * (nq + 1) // 2) % 2 == 0 and nq >= 2
    blk = pl.BlockSpec((None, s, LANES), lambda b, p: (b, 0, p))
    pair = lambda width, dtype: [pltpu.VMEM((2, tq, width), dtype)] * 2
    return pl.pallas_call(
        _attn_kernel,
        grid=(bsz, d_sb // LANES),
        in_specs=[blk, blk, blk, pl.BlockSpec(tri.shape, lambda b, p: (0, 0))],
        out_specs=blk,
        out_shape=jax.ShapeDtypeStruct((bsz, s, d_sb), _F32),
        scratch_shapes=[pltpu.VMEM((tq, tq), jnp.int32)]
        + pair(tq, _F32) + pair(2 * tq, _BF16) + pair(LANES, _F32)
        + pair(tq, _BF16) + pair(LANES, _F32)
        + [pltpu.VMEM((2, tq, LANES), _F32)] * 2,
        compiler_params=pltpu.CompilerParams(
            dimension_semantics=("arbitrary", "arbitrary"), vmem_limit_bytes=VMEM_LIMIT_BYTES),
        name="attn",
    )(q, k, v, tri)


def _out_ffn_kernel(x_ref, yab_ref, oc_ref, goc_ref, wo_ref, gf_ref, wu_ref, fcw_ref, fcb_ref, wd_ref,
                    gfin_ref, out_ref, ubuf_ref, carry_ref, acc_ref, *, final_norm):
    t = pl.program_id(1)
    tm = x_ref.shape[0]
    n_chunks = wu_ref.shape[0]
    ch = wd_ref.shape[1]

    oc = oc_ref[...]
    yc = (oc * _rms_scale(oc) * goc_ref[...]).astype(_BF16)
    y = jnp.concatenate([yab_ref[...], yc], axis=1)
    x1 = x_ref[...] + _dot(y, wo_ref[...])
    h = (x1 * _rms_scale(x1) * gf_ref[...]).astype(_BF16)

    @pl.when(t == 0)
    def _():
        carry_ref[...] = jnp.zeros(carry_ref.shape, _F32)

    acc_ref[...] = x1

    def chunk(c, _):
        up = _dot(h, wu_ref[c])
        ubuf_ref[0:FFN_HALO, :] = carry_ref[c]
        ubuf_ref[FFN_HALO:FFN_HALO + tm, :] = up
        carry_ref[c] = up[tm - FFN_HALO:tm, :]
        w = fcw_ref[c]
        conv = jnp.broadcast_to(fcb_ref[c], (tm, 2 * ch))
        for kk in range(FFN_CONV_K):
            off = FFN_HALO - (FFN_CONV_K - 1) + kk
            conv = conv + w[kk:kk + 1, :] * ubuf_ref[off:off + tm, :]
        gate = conv[:, :ch]
        act = (gate * jax.nn.sigmoid(gate) * conv[:, ch:]).astype(_BF16)
        acc_ref[...] += _dot(act, wd_ref[c])
        return 0

    lax.fori_loop(0, n_chunks, chunk, 0)
    x2 = acc_ref[...]
    if final_norm:
        x2 = x2 * _rms_scale(x2) * gfin_ref[...]
    out_ref[...] = x2


def _out_ffn(x, yab, oc, g_out_c, w_out, g_ffn, w_up_r, fcw_r, fcb_r, w_down_r, g_final, final_norm):
    bsz, s, d = x.shape
    tm = FFN_TILE
    assert s % tm == 0
    n_chunks, _, ch2 = w_up_r.shape
    const = lambda shape: pl.BlockSpec(shape, lambda b, t: (0,) * len(shape))
    tok = lambda width: pl.BlockSpec((None, tm, width), lambda b, t: (b, t, 0))
    kern = functools.partial(_out_ffn_kernel, final_norm=final_norm)
    return pl.pallas_call(
        kern,
        grid=(bsz, s // tm),
        in_specs=[tok(d), tok(yab.shape[-1]), tok(oc.shape[-1]), const(g_out_c.shape), const(w_out.shape),
                  const(g_ffn.shape), const(w_up_r.shape), const(fcw_r.shape), const(fcb_r.shape),
                  const(w_down_r.shape), const(g_final.shape)],
        out_specs=tok(d),
        out_shape=jax.ShapeDtypeStruct((bsz, s, d), _F32),
        scratch_shapes=[pltpu.VMEM((FFN_HALO + tm, ch2), _F32),
                        pltpu.VMEM((n_chunks, FFN_HALO, ch2), _F32),
                        pltpu.VMEM((tm, d), _F32)],
        compiler_params=pltpu.CompilerParams(
            dimension_semantics=("arbitrary", "arbitrary"), vmem_limit_bytes=VMEM_LIMIT_BYTES),
        name="out_ffn",
    )(x, yab, oc, g_out_c, w_out, g_ffn, w_up_r, fcw_r, fcb_r, w_down_r, g_final)


def kernel(x, g_mix, w_in, conv_w, conv_b, conv_ln_g, conv_ln_b, sgu_ln_g, sgu_ln_b, sgu_w, sgu_b,
           g_out, w_out, g_ffn, w_up, ffn_conv_w, ffn_conv_b, w_down, g_final):
    depth = w_in.shape[0]
    d_conv = conv_w.shape[-1]
    d_sgu = sgu_ln_g.shape[-1]
    d_sb = g_out.shape[-1] - d_conv - d_sgu
    d_ff = w_down.shape[1]
    n_chunks = d_ff // FFN_CHUNK
    assert d_ff % FFN_CHUNK == 0

    r = lax.broadcasted_iota(jnp.int32, (ATT_TILE, ATT_TILE), 0)
    c = lax.broadcasted_iota(jnp.int32, (ATT_TILE, ATT_TILE), 1)
    tri = jnp.where(r > c, 1.0, 0.0).astype(_BF16)
    tri = jnp.concatenate([tri, tri], axis=0)

    def chunked(a):
        g = a[..., :d_ff].reshape(a.shape[:-1] + (n_chunks, FFN_CHUNK))
        v = a[..., d_ff:].reshape(a.shape[:-1] + (n_chunks, FFN_CHUNK))
        return jnp.moveaxis(jnp.concatenate([g, v], axis=-1), -2, 0)

    row2 = lambda a: a.reshape(1, -1)
    for l in range(depth):
        sgu_bias = jnp.repeat(sgu_b[l].T, HEAD_DIM, axis=1)
        yab, q, k, v = _mix_in(
            x, row2(g_mix[l]), w_in[l].astype(_BF16), conv_w[l], row2(conv_b[l]), row2(conv_ln_g[l]),
            row2(conv_ln_b[l]), row2(sgu_ln_g[l]), row2(sgu_ln_b[l]), sgu_w[l], sgu_bias,
            row2(g_out[l, :d_conv + d_sgu]), d_conv, d_sgu, d_sb)
        oc = _attn(q, k, v, tri)
        x = _out_ffn(
            x, yab, oc, row2(g_out[l, d_conv + d_sgu:]), w_out[l].astype(_BF16), row2(g_ffn[l]),
            chunked(w_up[l].astype(_BF16)), chunked(ffn_conv_w[l]), chunked(ffn_conv_b[l][None, :]),
            w_down[l].astype(_BF16).reshape(n_chunks, FFN_CHUNK, -1), row2(g_final),
            final_norm=(l == depth - 1))
    return x
```
